```python
import jax, jax.numpy as jnp
from jax import lax
import numpy as np

D_MODEL = 2048
BATCH = 4
SEQ = 4096
DEPTH = 4

N_MIXERS = 2
N_A = (DEPTH + 1) // 2
N_B = DEPTH // 2
CONF_KERNEL = 31
SHORT_KERNEL = 3
D_FF = 4 * D_MODEL
N_MOD = 6
RMS_EPS = 1e-6
LN_EPS = 1e-5

kernel_name = "hybrid_conformer_shortconv_adaln_trunk"


def rmsnorm(x, g):
    xf = x.astype(jnp.float32)
    y = xf * lax.rsqrt(jnp.mean(xf * xf, axis=-1, keepdims=True) + RMS_EPS)
    return (y * g.astype(jnp.float32)).astype(x.dtype)


def layernorm(x, g, b):
    xf = x.astype(jnp.float32)
    mu = jnp.mean(xf, axis=-1, keepdims=True)
    var = jnp.mean(jnp.square(xf - mu), axis=-1, keepdims=True)
    y = (xf - mu) * lax.rsqrt(var + LN_EPS)
    return (y * g.astype(jnp.float32) + b.astype(jnp.float32)).astype(x.dtype)


def causal_depthwise_conv(u, w):
    k = w.shape[0]
    return lax.conv_general_dilated(
        u, w[:, None, :].astype(u.dtype),
        window_strides=(1,), padding=((k - 1, 0),),
        dimension_numbers=("NWC", "WIO", "NWC"),
        feature_group_count=u.shape[-1])


def conformer_conv_module(h, w1, b1, dw, dwb, ln_g, ln_b, w2, b2):
    u = jnp.einsum("bsd,de->bse", h, w1) + b1
    a, g = jnp.split(u, 2, axis=-1)
    u = a * jax.nn.sigmoid(g)
    u = causal_depthwise_conv(u, dw) + dwb
    u = jax.nn.silu(layernorm(u, ln_g, ln_b))
    return jnp.einsum("bsd,de->bse", u, w2) + b2


def short_gated_conv(h, w_in, w_conv, w_out):
    z = jnp.einsum("bsd,de->bse", h, w_in)
    gate_b, gate_c, xin = jnp.split(z, 3, axis=-1)
    u = causal_depthwise_conv(gate_c * xin, w_conv)
    return jnp.einsum("bsd,de->bse", gate_b * u, w_out)


def sq_relu_mlp(h, w1, w2):
    u = jnp.square(jax.nn.relu(jnp.einsum("bsd,df->bsf", h, w1)))
    return jnp.einsum("bsf,fd->bsd", u, w2)


def setup_inputs(seed: int = 0) -> dict:
    key = jax.random.key(seed)
    ks = jax.random.split(key, 24)
    D = D_MODEL
    f32 = jnp.float32

    def nrm(k, shape, scale):
        return jax.random.normal(k, shape, f32) * scale

    def gain(k, shape):
        return 1.0 + 0.05 * jax.random.normal(k, shape, f32)

    return {
        "x": nrm(ks[0], (BATCH, SEQ, D), 1.0),
        "c": nrm(ks[1], (BATCH, D), 1.0),
        "mod_w": nrm(ks[2], (DEPTH, D, N_MOD * D), 0.5 * D ** -0.5),
        "mod_b": nrm(ks[3], (DEPTH, N_MOD * D), 0.02),
        "pre_mix_g": gain(ks[4], (DEPTH, D)),
        "post_mix_g": gain(ks[5], (DEPTH, D)),
        "pre_ffn_g": gain(ks[6], (DEPTH, D)),
        "post_ffn_g": gain(ks[7], (DEPTH, D)),
        "a_w1": nrm(ks[8], (N_A, D, 2 * D), D ** -0.5),
        "a_b1": nrm(ks[9], (N_A, 2 * D), 0.02),
        "a_dw": nrm(ks[10], (N_A, CONF_KERNEL, D), CONF_KERNEL ** -0.5),
        "a_dwb": nrm(ks[11], (N_A, D), 0.02),
        "a_ln_g": gain(ks[12], (N_A, D)),
        "a_ln_b": nrm(ks[13], (N_A, D), 0.02),
        "a_w2": nrm(ks[14], (N_A, D, D), D ** -0.5),
        "a_b2": nrm(ks[15], (N_A, D), 0.02),
        "b_w_in": nrm(ks[16], (N_B, D, 3 * D), D ** -0.5),
        "b_conv": nrm(ks[17], (N_B, SHORT_KERNEL, D), SHORT_KERNEL ** -0.5),
        "b_w_out": nrm(ks[18], (N_B, D, D), D ** -0.5),
        "f_w1": nrm(ks[19], (DEPTH, D, D_FF), D ** -0.5),
        "f_w2": nrm(ks[20], (DEPTH, D_FF, D), D_FF ** -0.5),
    }


def reference(x, c, mod_w, mod_b, pre_mix_g, post_mix_g, pre_ffn_g, post_ffn_g,
              a_w1, a_b1, a_dw, a_dwb, a_ln_g, a_ln_b, a_w2, a_b2,
              b_w_in, b_conv, b_w_out, f_w1, f_w2):
    c_act = jax.nn.silu(c)
    for i in range(DEPTH):
        mod = jnp.einsum("bd,de->be", c_act, mod_w[i]) + mod_b[i]
        sh_m, sc_m, gt_m, sh_f, sc_f, gt_f = [m[:, None, :] for m in jnp.split(mod, N_MOD, axis=-1)]

        h = rmsnorm(x, pre_mix_g[i]) * (1.0 + sc_m) + sh_m
        j = i // N_MIXERS
        if i % N_MIXERS == 0:
            y = conformer_conv_module(h, a_w1[j], a_b1[j], a_dw[j], a_dwb[j],
                                      a_ln_g[j], a_ln_b[j], a_w2[j], a_b2[j])
        else:
            y = short_gated_conv(h, b_w_in[j], b_conv[j], b_w_out[j])
        x = x + gt_m * rmsnorm(y, post_mix_g[i])

        h = rmsnorm(x, pre_ffn_g[i]) * (1.0 + sc_f) + sh_f
        y = sq_relu_mlp(h, f_w1[i], f_w2[i])
        x = x + gt_f * rmsnorm(y, post_ffn_g[i])
    return x
```

```python
import functools

import jax
import jax.numpy as jnp
from jax import lax
from jax.experimental import pallas as pl
from jax.experimental.pallas import tpu as pltpu

RMS_EPS = 1e-6
LN_EPS = 1e-5
N_MOD = 6

F32 = jnp.float32
BF16 = jnp.bfloat16

TOKEN_TILE = 512
CHANNEL_CHUNK = 512
FF_CHUNK = 1024
MOD_CHUNK = 1024
CONV_ROWS = 32
CONF_HALO = 32
SHORT_HALO = 8
VMEM_LIMIT_BYTES = 56 * 1024 * 1024

_ARBITRARY3 = ("arbitrary", "arbitrary", "arbitrary")


def _pre_norm_mod(x, g, shift, scale):
    ms = jnp.mean(x * x, axis=-1, keepdims=True)
    return (x * lax.rsqrt(ms + RMS_EPS)) * (g * (1.0 + scale)) + shift


def _post_norm_residual(x, y, g, gate):
    ms = jnp.mean(y * y, axis=-1, keepdims=True)
    return x + (gate * g) * (y * lax.rsqrt(ms + RMS_EPS))


def _sigmoid(v):
    return 1.0 / (1.0 + jnp.exp(-v))


def _mod_kernel(c_ref, w_ref, b_ref, o_ref):
    c = c_ref[...]
    c_act = (c * _sigmoid(c)).astype(BF16)
    w = w_ref[...].astype(BF16)
    o_ref[...] = jnp.dot(c_act, w, preferred_element_type=F32) + b_ref[...]


def _modulation(c, mod_w, mod_b):
    depth, d, n = mod_w.shape
    b = c.shape[0]
    rows = 8
    c_pad = jnp.zeros((rows, d), F32).at[:b].set(c)
    out = pl.pallas_call(
        _mod_kernel,
        grid=(depth, n // MOD_CHUNK),
        in_specs=[
            pl.BlockSpec((rows, d), lambda i, k: (0, 0)),
            pl.BlockSpec((None, d, MOD_CHUNK), lambda i, k: (i, 0, k)),
            pl.BlockSpec((None, 1, MOD_CHUNK), lambda i, k: (i, 0, k)),
        ],
        out_specs=pl.BlockSpec((None, rows, MOD_CHUNK), lambda i, k: (i, 0, k)),
        out_shape=jax.ShapeDtypeStruct((depth, rows, n), F32),
        compiler_params=pltpu.CompilerParams(
            dimension_semantics=("arbitrary", "arbitrary"),
            vmem_limit_bytes=VMEM_LIMIT_BYTES),
        name="adaln_modulation",
    )(c_pad, mod_w, mod_b.reshape(depth, 1, n))
    return out[:, :b].reshape(depth, b, N_MOD, d)


def _mlp_kernel(x_ref, mod_ref, pre_g_ref, post_g_ref, w1_ref, w2_ref,
                o_ref, h_ref):
    j = pl.program_id(2)

    @pl.when(j == 0)
    def _():
        h = _pre_norm_mod(x_ref[...], pre_g_ref[...], mod_ref[3:4, :], mod_ref[4:5, :])
        h_ref[...] = h.astype(BF16)
        o_ref[...] = jnp.zeros_like(o_ref)

    u = jnp.dot(h_ref[...], w1_ref[...], preferred_element_type=F32)
    u = jnp.maximum(u, 0.0)
    u = (u * u).astype(BF16)
    o_ref[...] += jnp.dot(u, w2_ref[...], preferred_element_type=F32)

    @pl.when(j == pl.num_programs(2) - 1)
    def _():
        o_ref[...] = _post_norm_residual(x_ref[...], o_ref[...], post_g_ref[...],
                                         mod_ref[5:6, :])


def _mlp_sublayer(x, mod, pre_g, post_g, w1, w2):
    b, s, d = x.shape
    dff = w1.shape[1]
    tm, tf = TOKEN_TILE, FF_CHUNK
    row = lambda bi, ti, ji: (0, 0)
    return pl.pallas_call(
        _mlp_kernel,
        grid=(b, s // tm, dff // tf),
        in_specs=[
            pl.BlockSpec((None, tm, d), lambda bi, ti, ji: (bi, ti, 0)),
            pl.BlockSpec((None, N_MOD, d), lambda bi, ti, ji: (bi, 0, 0)),
            pl.BlockSpec((1, d), row),
            pl.BlockSpec((1, d), row),
            pl.BlockSpec((d, tf), lambda bi, ti, ji: (0, ji)),
            pl.BlockSpec((tf, d), lambda bi, ti, ji: (ji, 0)),
        ],
        out_specs=pl.BlockSpec((None, tm, d), lambda bi, ti, ji: (bi, ti, 0)),
        out_shape=jax.ShapeDtypeStruct(x.shape, F32),
        scratch_shapes=[pltpu.VMEM((tm, d), BF16)],
        compiler_params=pltpu.CompilerParams(
            dimension_semantics=_ARBITRARY3, vmem_limit_bytes=VMEM_LIMIT_BYTES),
        name="mlp_sublayer",
    )(x, mod, pre_g.reshape(1, d), post_g.reshape(1, d), w1, w2)


def _sconv_kernel(x_ref, mod_ref, pre_g_ref, post_g_ref, wb_ref, wc_ref, wx_ref,
                  cw_ref, wo_ref, o_ref, h_ref, v_ref, tail_ref):
    t = pl.program_id(1)
    j = pl.program_id(2)
    tm = x_ref.shape[0]
    halo = SHORT_HALO

    @pl.when(j == 0)
    def _():
        h = _pre_norm_mod(x_ref[...], pre_g_ref[...], mod_ref[0:1, :], mod_ref[1:2, :])
        h_ref[...] = h.astype(BF16)
        o_ref[...] = jnp.zeros_like(o_ref)

    h = h_ref[...]
    v = (jnp.dot(h, wc_ref[...], preferred_element_type=F32)
         * jnp.dot(h, wx_ref[...], preferred_element_type=F32))

    @pl.when(t == 0)
    def _():
        v_ref[0:halo, :] = jnp.zeros((halo, v_ref.shape[1]), F32)

    @pl.when(t > 0)
    def _():
        v_ref[0:halo, :] = tail_ref[j]

    v_ref[halo:, :] = v
    tail_ref[j] = v[tm - halo:, :]

    u = (cw_ref[0:1, :] * v_ref[halo - 2:halo - 2 + tm, :]
         + cw_ref[1:2, :] * v_ref[halo - 1:halo - 1 + tm, :]
         + cw_ref[2:3, :] * v)
    p = (jnp.dot(h, wb_ref[...], preferred_element_type=F32) * u).astype(BF16)
    o_ref[...] += jnp.dot(p, wo_ref[...], preferred_element_type=F32)

    @pl.when(j == pl.num_programs(2) - 1)
    def _():
        o_ref[...] = _post_norm_residual(x_ref[...], o_ref[...], post_g_ref[...],
                                         mod_ref[2:3, :])


def _sconv_sublayer(x, mod, pre_g, post_g, w_in, w_conv, w_out):
    b, s, d = x.shape
    tm, tc = TOKEN_TILE, CHANNEL_CHUNK
    nj = d // tc
    taps = w_conv.shape[0]
    row = lambda bi, ti, ji: (0, 0)
    return pl.pallas_call(
        _sconv_kernel,
        grid=(b, s // tm, nj),
        in_specs=[
            pl.BlockSpec((None, tm, d), lambda bi, ti, ji: (bi, ti, 0)),
            pl.BlockSpec((None, N_MOD, d), lambda bi, ti, ji: (bi, 0, 0)),
            pl.BlockSpec((1, d), row),
            pl.BlockSpec((1, d), row),
            pl.BlockSpec((d, tc), lambda bi, ti, ji: (0, ji)),
            pl.BlockSpec((d, tc), lambda bi, ti, ji: (0, nj + ji)),
            pl.BlockSpec((d, tc), lambda bi, ti, ji: (0, 2 * nj + ji)),
            pl.BlockSpec((taps, tc), lambda bi, ti, ji: (0, ji)),
            pl.BlockSpec((tc, d), lambda bi, ti, ji: (ji, 0)),
        ],
        out_specs=pl.BlockSpec((None, tm, d), lambda bi, ti, ji: (bi, ti, 0)),
        out_shape=jax.ShapeDtypeStruct(x.shape, F32),
        scratch_shapes=[
            pltpu.VMEM((tm, d), BF16),
            pltpu.VMEM((SHORT_HALO + tm, tc), F32),
            pltpu.VMEM((nj, SHORT_HALO, tc), F32),
        ],
        compiler_params=pltpu.CompilerParams(
            dimension_semantics=_ARBITRARY3, vmem_limit_bytes=VMEM_LIMIT_BYTES),
        name="short_conv_sublayer",
    )(x, mod, pre_g.reshape(1, d), post_g.reshape(1, d), w_in, w_in, w_in, w_conv, w_out)


def _conf_kernel(x_ref, mod_ref, pre_g_ref, post_g_ref, w1a_ref, w1g_ref,
                 b1a_ref, b1g_ref, dw_ref, dwb_ref, lng_ref, lnb_ref, w2_ref, b2_ref,
                 o_ref, h_ref, gx_ref, tail_ref, cv_ref, un_ref):
    t = pl.program_id(1)
    j = pl.program_id(2)
    tm, d = x_ref.shape
    nj, _, tc = cv_ref.shape
    halo = CONF_HALO
    taps = dw_ref.shape[0]

    @pl.when(j == 0)
    def _():
        h = _pre_norm_mod(x_ref[...], pre_g_ref[...], mod_ref[0:1, :], mod_ref[1:2, :])
        h_ref[...] = h.astype(BF16)

    h = h_ref[...]
    a = jnp.dot(h, w1a_ref[...], preferred_element_type=F32) + b1a_ref[...]
    g = jnp.dot(h, w1g_ref[...], preferred_element_type=F32) + b1g_ref[...]

    @pl.when(t == 0)
    def _():
        gx_ref[0:halo, :] = jnp.zeros((halo, tc), F32)

    @pl.when(t > 0)
    def _():
        gx_ref[0:halo, :] = tail_ref[j]

    gx_ref[halo:, :] = a * _sigmoid(g)
    tail_ref[j] = gx_ref[tm:tm + halo, :]

    base = halo - (taps - 1)

    for r0 in range(0, tm, CONV_ROWS):
        acc = jnp.zeros((CONV_ROWS, tc), F32)
        for k in range(taps):
            acc = acc + dw_ref[k:k + 1, :] * gx_ref[r0 + base + k:r0 + base + k + CONV_ROWS, :]
        cv_ref[j, r0:r0 + CONV_ROWS, :] = acc + dwb_ref[...]

    @pl.when(j == nj - 1)
    def _():
        total = jnp.zeros((tm, 1), F32)
        for jj in range(nj):
            total = total + jnp.sum(cv_ref[jj], axis=-1, keepdims=True)
        mu = total / d
        sq = jnp.zeros((tm, 1), F32)
        for jj in range(nj):
            dev = cv_ref[jj] - mu
            sq = sq + jnp.sum(dev * dev, axis=-1, keepdims=True)
        rstd = lax.rsqrt(sq / d + LN_EPS)
        for jj in range(nj):
            cols = slice(jj * tc, (jj + 1) * tc)
            z = (cv_ref[jj] - mu) * rstd * lng_ref[:, cols] + lnb_ref[:, cols]
            un_ref[:, cols] = (z * _sigmoid(z)).astype(BF16)
        y = jnp.dot(un_ref[...], w2_ref[...], preferred_element_type=F32) + b2_ref[...]
        o_ref[...] = _post_norm_residual(x_ref[...], y, post_g_ref[...], mod_ref[2:3, :])


def _conf_sublayer(x, mod, pre_g, post_g, w1, b1, dw, dwb, ln_g, ln_b, w2, b2):
    b, s, d = x.shape
    tm, tc = TOKEN_TILE, CHANNEL_CHUNK
    nj = d // tc
    taps = dw.shape[0]
    row = lambda bi, ti, ji: (0, 0)
    chunk = lambda bi, ti, ji: (0, ji)
    gate_chunk = lambda bi, ti, ji: (0, nj + ji)
    b1r = b1.reshape(1, 2 * d)
    return pl.pallas_call(
        _conf_kernel,
        grid=(b, s // tm, nj),
        in_specs=[
            pl.BlockSpec((None, tm, d), lambda bi, ti, ji: (bi, ti, 0)),
            pl.BlockSpec((None, N_MOD, d), lambda bi, ti, ji: (bi, 0, 0)),
            pl.BlockSpec((1, d), row),
            pl.BlockSpec((1, d), row),
            pl.BlockSpec((d, tc), chunk),
            pl.BlockSpec((d, tc), gate_chunk),
            pl.BlockSpec((1, tc), chunk),
            pl.BlockSpec((1, tc), gate_chunk),
            pl.BlockSpec((taps, tc), chunk),
            pl.BlockSpec((1, tc), chunk),
            pl.BlockSpec((1, d), row),
            pl.BlockSpec((1, d), row),
            pl.BlockSpec((d, d), row, pipeline_mode=pl.Buffered(1)),
            pl.BlockSpec((1, d), row),
        ],
        out_specs=pl.BlockSpec((None, tm, d), lambda bi, ti, ji: (bi, ti, 0)),
        out_shape=jax.ShapeDtypeStruct(x.shape, F32),
        scratch_shapes=[
            pltpu.VMEM((tm, d), BF16),
            pltpu.VMEM((CONF_HALO + tm, tc), F32),
            pltpu.VMEM((nj, CONF_HALO, tc), F32),
            pltpu.VMEM((nj, tm, tc), F32),
            pltpu.VMEM((tm, d), BF16),
        ],
        compiler_params=pltpu.CompilerParams(
            dimension_semantics=_ARBITRARY3, vmem_limit_bytes=VMEM_LIMIT_BYTES),
        name="conformer_sublayer",
    )(x, mod, pre_g.reshape(1, d), post_g.reshape(1, d), w1, w1, b1r, b1r, dw,
      dwb.reshape(1, d), ln_g.reshape(1, d), ln_b.reshape(1, d), w2, b2.reshape(1, d))


def kernel(x, c, mod_w, mod_b, pre_mix_g, post_mix_g, pre_ffn_g, post_ffn_g, a_w1, a_b1, a_dw, a_dwb, a_ln_g, a_ln_b, a_w2, a_b2, b_w_in, b_conv, b_w_out, f_w1, f_w2):
    depth = mod_w.shape[0]
    mod = _modulation(c, mod_w, mod_b)
    for i in range(depth):
        j = i // 2
        if i % 2 == 0:
            x = _conf_sublayer(x, mod[i], pre_mix_g[i], post_mix_g[i],
                               a_w1[j].astype(BF16), a_b1[j], a_dw[j], a_dwb[j],
                               a_ln_g[j], a_ln_b[j], a_w2[j].astype(BF16), a_b2[j])
        else:
            x = _sconv_sublayer(x, mod[i], pre_mix_g[i], post_mix_g[i],
                                b_w_in[j].astype(BF16), b_conv[j], b_w_out[j].astype(BF16))
        x = _mlp_sublayer(x, mod[i], pre_ffn_g[i], post_ffn_g[i],
                          f_w1[i].astype(BF16), f_w2[i].astype(BF16))
    return x
```

```python
import jax
import jax.numpy as jnp
from jax import lax
from jax.experimental import pallas as pl
from jax.experimental.pallas import tpu as pltpu

RMS_EPS = 1e-6
LN_EPS = 1e-5
N_MOD = 6

F32 = jnp.float32
BF16 = jnp.bfloat16

LANES = 128
TOKEN_TILE = 512
CHANNEL_CHUNK = 512
FF_CHUNK = 1024
MOD_CHUNK = 1024
NORM_ROWS = 64
CONV_ROWS = 32
CONF_HALO = 32
SHORT_HALO = 8
VMEM_LIMIT_BYTES = 58 * 1024 * 1024


def _sigmoid(v):
    return 1.0 / (1.0 + jnp.exp(-v))


def _pre_norm_mod(h_ref, x_ref, gs_ref, pre_g_ref, shift_ref, scale_ref):
    gs_ref[...] = pre_g_ref[...] * (1.0 + scale_ref[...])
    for r0 in range(0, x_ref.shape[0], NORM_ROWS):
        x = x_ref[r0:r0 + NORM_ROWS, :]
        ms = jnp.mean(x * x, axis=-1, keepdims=True)
        h = (x * lax.rsqrt(ms + RMS_EPS)) * gs_ref[...] + shift_ref[...]
        h_ref[r0:r0 + NORM_ROWS, :] = h.astype(BF16)


def _post_norm_residual(o_ref, x_ref, y_ref, gs_ref, post_g_ref, gate_ref):
    gs_ref[...] = post_g_ref[...] * gate_ref[...]
    for r0 in range(0, x_ref.shape[0], NORM_ROWS):
        y = y_ref[r0:r0 + NORM_ROWS, :]
        ms = jnp.mean(y * y, axis=-1, keepdims=True)
        o_ref[r0:r0 + NORM_ROWS, :] = (x_ref[r0:r0 + NORM_ROWS, :]
                                       + gs_ref[...] * (y * lax.rsqrt(ms + RMS_EPS)))


def _mod_kernel(c_ref, w_ref, b_ref, o_ref):
    c = c_ref[...]
    c_act = (c * _sigmoid(c)).astype(BF16)
    w = w_ref[...].astype(BF16)
    o_ref[...] = jnp.dot(c_act, w, preferred_element_type=F32) + b_ref[...]


def _modulation(c, mod_w, mod_b):
    depth, d, n = mod_w.shape
    b = c.shape[0]
    rows = 8
    c_pad = jnp.zeros((rows, d), F32).at[:b].set(c)
    out = pl.pallas_call(
        _mod_kernel,
        grid=(depth, n // MOD_CHUNK),
        in_specs=[
            pl.BlockSpec((rows, d), lambda i, k: (0, 0)),
            pl.BlockSpec((None, d, MOD_CHUNK), lambda i, k: (i, 0, k)),
            pl.BlockSpec((None, 1, MOD_CHUNK), lambda i, k: (i, 0, k)),
        ],
        out_specs=pl.BlockSpec((None, rows, MOD_CHUNK), lambda i, k: (i, 0, k)),
        out_shape=jax.ShapeDtypeStruct((depth, rows, n), F32),
        compiler_params=pltpu.CompilerParams(
            dimension_semantics=("arbitrary", "arbitrary"),
            vmem_limit_bytes=VMEM_LIMIT_BYTES),
        name="adaln_modulation",
    )(c_pad, mod_w, mod_b.reshape(depth, 1, n))
    return out[:, :b].reshape(depth, b, N_MOD, d)


def _mlp_kernel(x_ref, mod_ref, pre_g_ref, post_g_ref, w1_ref, w2_ref,
                o_ref, h_ref, gs_ref):
    j = pl.program_id(2)

    @pl.when(j == 0)
    def _():
        _pre_norm_mod(h_ref, x_ref, gs_ref, pre_g_ref, mod_ref.at[3:4, :], mod_ref.at[4:5, :])
        o_ref[...] = jnp.zeros_like(o_ref)

    u = jnp.dot(h_ref[...], w1_ref[...], preferred_element_type=F32)
    u = jnp.maximum(u, 0.0)
    u = (u * u).astype(BF16)
    o_ref[...] += jnp.dot(u, w2_ref[...], preferred_element_type=F32)

    @pl.when(j == pl.num_programs(2) - 1)
    def _():
        _post_norm_residual(o_ref, x_ref, o_ref, gs_ref, post_g_ref, mod_ref.at[5:6, :])


def _mlp_sublayer(x, mod, pre_g, post_g, w1, w2, layer):
    b, s, d = x.shape
    dff = w1.shape[-1]
    tm, tf = TOKEN_TILE, FF_CHUNK
    row = lambda bi, ti, ji: (layer, 0, 0)
    return pl.pallas_call(
        _mlp_kernel,
        grid=(b, s // tm, dff // tf),
        in_specs=[
            pl.BlockSpec((None, tm, d), lambda bi, ti, ji: (bi, ti, 0)),
            pl.BlockSpec((None, None, N_MOD, d), lambda bi, ti, ji: (layer, bi, 0, 0)),
            pl.BlockSpec((None, 1, d), row),
            pl.BlockSpec((None, 1, d), row),
            pl.BlockSpec((None, d, tf), lambda bi, ti, ji: (layer, 0, ji)),
            pl.BlockSpec((None, tf, d), lambda bi, ti, ji: (layer, ji, 0)),
        ],
        out_specs=pl.BlockSpec((None, tm, d), lambda bi, ti, ji: (bi, ti, 0)),
        out_shape=jax.ShapeDtypeStruct(x.shape, F32),
        scratch_shapes=[pltpu.VMEM((tm, d), BF16), pltpu.VMEM((1, d), F32)],
        compiler_params=pltpu.CompilerParams(
            dimension_semantics=("arbitrary", "arbitrary", "arbitrary"),
            vmem_limit_bytes=VMEM_LIMIT_BYTES),
        name="mlp_sublayer",
    )(x, mod, pre_g, post_g, w1, w2)


def _sconv_kernel(x_ref, mod_ref, pre_g_ref, post_g_ref, wb_ref, wc_ref, wx_ref,
                  cw_ref, wo_ref, o_ref, h_ref, gs_ref, v_ref, tail_ref, p_ref):
    t = pl.program_id(1)
    j = pl.program_id(2)
    tm = x_ref.shape[0]
    halo = SHORT_HALO
    nl = v_ref.shape[0]

    @pl.when(j == 0)
    def _():
        _pre_norm_mod(h_ref, x_ref, gs_ref, pre_g_ref, mod_ref.at[0:1, :], mod_ref.at[1:2, :])
        o_ref[...] = jnp.zeros_like(o_ref)

    @pl.when(t == 0)
    def _():
        tail_ref[j] = jnp.zeros(tail_ref.shape[1:], F32)

    h = h_ref[...]
    v = (jnp.dot(h, wc_ref[...], preferred_element_type=F32)
         * jnp.dot(h, wx_ref[...], preferred_element_type=F32))
    zb = jnp.dot(h, wb_ref[...], preferred_element_type=F32)

    for c in range(nl):
        cols = slice(c * LANES, (c + 1) * LANES)
        vc = v[:, cols]
        v_ref[c, 0:halo, :] = tail_ref[j, c]
        v_ref[c, halo:, :] = vc
        tail_ref[j, c] = vc[tm - halo:, :]
        u = (cw_ref[0:1, cols] * v_ref[c, halo - 2:halo - 2 + tm, :]
             + cw_ref[1:2, cols] * v_ref[c, halo - 1:halo - 1 + tm, :]
             + cw_ref[2:3, cols] * vc)
        p_ref[:, cols] = (zb[:, cols] * u).astype(BF16)

    o_ref[...] += jnp.dot(p_ref[...], wo_ref[...], preferred_element_type=F32)

    @pl.when(j == pl.num_programs(2) - 1)
    def _():
        _post_norm_residual(o_ref, x_ref, o_ref, gs_ref, post_g_ref, mod_ref.at[2:3, :])


def _sconv_sublayer(x, mod, pre_g, post_g, w_in, w_conv, w_out, layer, mixer):
    b, s, d = x.shape
    tm, tc = TOKEN_TILE, CHANNEL_CHUNK
    nj = d // tc
    nl = tc // LANES
    taps = w_conv.shape[1]
    row = lambda bi, ti, ji: (layer, 0, 0)
    return pl.pallas_call(
        _sconv_kernel,
        grid=(b, s // tm, nj),
        in_specs=[
            pl.BlockSpec((None, tm, d), lambda bi, ti, ji: (bi, ti, 0)),
            pl.BlockSpec((None, None, N_MOD, d), lambda bi, ti, ji: (layer, bi, 0, 0)),
            pl.BlockSpec((None, 1, d), row),
            pl.BlockSpec((None, 1, d), row),
            pl.BlockSpec((None, d, tc), lambda bi, ti, ji: (mixer, 0, ji)),
            pl.BlockSpec((None, d, tc), lambda bi, ti, ji: (mixer, 0, nj + ji)),
            pl.BlockSpec((None, d, tc), lambda bi, ti, ji: (mixer, 0, 2 * nj + ji)),
            pl.BlockSpec((None, taps, tc), lambda bi, ti, ji: (mixer, 0, ji)),
            pl.BlockSpec((None, tc, d), lambda bi, ti, ji: (mixer, ji, 0)),
        ],
        out_specs=pl.BlockSpec((None, tm, d), lambda bi, ti, ji: (bi, ti, 0)),
        out_shape=jax.ShapeDtypeStruct(x.shape, F32),
        scratch_shapes=[
            pltpu.VMEM((tm, d), BF16),
            pltpu.VMEM((1, d), F32),
            pltpu.VMEM((nl, SHORT_HALO + tm, LANES), F32),
            pltpu.VMEM((nj, nl, SHORT_HALO, LANES), F32),
            pltpu.VMEM((tm, tc), BF16),
        ],
        compiler_params=pltpu.CompilerParams(
            dimension_semantics=("arbitrary", "arbitrary", "arbitrary"),
            vmem_limit_bytes=VMEM_LIMIT_BYTES),
        name="short_conv_sublayer",
    )(x, mod, pre_g, post_g, w_in, w_in, w_in, w_conv, w_out)


def _conf_kernel(x_ref, mod_ref, pre_g_ref, post_g_ref, w1_ref, b1_ref, dw_ref, dwb_ref,
                 lng_ref, lnb_ref, w2_ref, b2_ref, o_ref, h_ref, gs_ref, gx_ref):
    t = pl.program_id(1)
    tm, d = x_ref.shape
    nl = gx_ref.shape[0]
    halo = CONF_HALO
    taps = dw_ref.shape[0]
    tc = CHANNEL_CHUNK
    base = halo - (taps - 1)

    @pl.when(t == 0)
    def _():
        gx_ref[:, tm:tm + halo, :] = jnp.zeros((nl, halo, LANES), F32)

    _pre_norm_mod(h_ref, x_ref, gs_ref, pre_g_ref, mod_ref.at[0:1, :], mod_ref.at[1:2, :])

    for jj in range(d // tc):
        h = h_ref[...]
        a = (jnp.dot(h, w1_ref[:, jj * tc:(jj + 1) * tc], preferred_element_type=F32)
             + b1_ref[:, jj * tc:(jj + 1) * tc])
        g = (jnp.dot(h, w1_ref[:, d + jj * tc:d + (jj + 1) * tc], preferred_element_type=F32)
             + b1_ref[:, d + jj * tc:d + (jj + 1) * tc])
        glu = a * _sigmoid(g)
        for c in range(tc // LANES):
            lane = jj * (tc // LANES) + c
            cols = slice(lane * LANES, (lane + 1) * LANES)
            gx_ref[lane, 0:halo, :] = gx_ref[lane, tm:tm + halo, :]
            gx_ref[lane, halo:, :] = glu[:, c * LANES:(c + 1) * LANES]
            for r0 in range(0, tm, CONV_ROWS):
                acc = jnp.zeros((CONV_ROWS, LANES), F32)
                for k in range(taps):
                    acc = acc + dw_ref[k:k + 1, cols] * gx_ref[lane, r0 + base + k:r0 + base + k + CONV_ROWS, :]
                o_ref[r0:r0 + CONV_ROWS, cols] = acc + dwb_ref[:, cols]

    for r0 in range(0, tm, NORM_ROWS):
        cv = o_ref[r0:r0 + NORM_ROWS, :]
        mu = jnp.mean(cv, axis=-1, keepdims=True)
        dev = cv - mu
        var = jnp.mean(dev * dev, axis=-1, keepdims=True)
        z = dev * lax.rsqrt(var + LN_EPS) * lng_ref[...] + lnb_ref[...]
        h_ref[r0:r0 + NORM_ROWS, :] = (z * _sigmoid(z)).astype(BF16)

    o_ref[...] = jnp.dot(h_ref[...], w2_ref[...], preferred_element_type=F32) + b2_ref[...]
    _post_norm_residual(o_ref, x_ref, o_ref, gs_ref, post_g_ref, mod_ref.at[2:3, :])


def _conf_sublayer(x, mod, pre_g, post_g, w1, b1, dw, dwb, ln_g, ln_b, w2, b2, layer, mixer):
    b, s, d = x.shape
    tm = TOKEN_TILE
    taps = dw.shape[1]
    row = lambda bi, ti: (layer, 0, 0)
    mrow = lambda bi, ti: (mixer, 0, 0)
    resident = dict(pipeline_mode=pl.Buffered(1))
    return pl.pallas_call(
        _conf_kernel,
        grid=(b, s // tm),
        in_specs=[
            pl.BlockSpec((None, tm, d), lambda bi, ti: (bi, ti, 0)),
            pl.BlockSpec((None, None, N_MOD, d), lambda bi, ti: (layer, bi, 0, 0)),
            pl.BlockSpec((None, 1, d), row),
            pl.BlockSpec((None, 1, d), row),
            pl.BlockSpec((None, d, 2 * d), mrow, **resident),
            pl.BlockSpec((None, 1, 2 * d), mrow),
            pl.BlockSpec((None, taps, d), mrow),
            pl.BlockSpec((None, 1, d), mrow),
            pl.BlockSpec((None, 1, d), mrow),
            pl.BlockSpec((None, 1, d), mrow),
            pl.BlockSpec((None, d, d), mrow, **resident),
            pl.BlockSpec((None, 1, d), mrow),
        ],
        out_specs=pl.BlockSpec((None, tm, d), lambda bi, ti: (bi, ti, 0)),
        out_shape=jax.ShapeDtypeStruct(x.shape, F32),
        scratch_shapes=[
            pltpu.VMEM((tm, d), BF16),
            pltpu.VMEM((1, d), F32),
            pltpu.VMEM((d // LANES, CONF_HALO + tm, LANES), F32),
        ],
        compiler_params=pltpu.CompilerParams(
            dimension_semantics=("arbitrary", "arbitrary"),
            vmem_limit_bytes=VMEM_LIMIT_BYTES),
        name="conformer_sublayer",
    )(x, mod, pre_g, post_g, w1, b1, dw, dwb, ln_g, ln_b, w2, b2)


def kernel(x, c, mod_w, mod_b, pre_mix_g, post_mix_g, pre_ffn_g, post_ffn_g, a_w1, a_b1, a_dw, a_dwb, a_ln_g, a_ln_b, a_w2, a_b2, b_w_in, b_conv, b_w_out, f_w1, f_w2):
    depth = mod_w.shape[0]
    mod = _modulation(c, mod_w, mod_b)
    rows = lambda p: p[:, None, :]
    pre_mix_g, post_mix_g = rows(pre_mix_g), rows(post_mix_g)
    pre_ffn_g, post_ffn_g = rows(pre_ffn_g), rows(post_ffn_g)
    a_b1, a_dwb, a_ln_g, a_ln_b, a_b2 = map(rows, (a_b1, a_dwb, a_ln_g, a_ln_b, a_b2))
    a_w1, a_w2 = a_w1.astype(BF16), a_w2.astype(BF16)
    b_w_in, b_w_out = b_w_in.astype(BF16), b_w_out.astype(BF16)
    f_w1, f_w2 = f_w1.astype(BF16), f_w2.astype(BF16)
    for i in range(depth):
        j = i // 2
        if i % 2 == 0:
            x = _conf_sublayer(x, mod, pre_mix_g, post_mix_g, a_w1, a_b1, a_dw, a_dwb,
                               a_ln_g, a_ln_b, a_w2, a_b2, i, j)
        else:
            x = _sconv_sublayer(x, mod, pre_mix_g, post_mix_g, b_w_in, b_conv, b_w_out, i, j)
        x = _mlp_sublayer(x, mod, pre_ffn_g, post_ffn_g, f_w1, f_w2, i)
    return x
```

```python
import functools

import jax
import jax.numpy as jnp
from jax import lax
from jax.experimental import pallas as pl
from jax.experimental.pallas import tpu as pltpu

RMS_EPS = 1e-6
LN_EPS = 1e-5
N_MOD = 6

F32 = jnp.float32
BF16 = jnp.bfloat16

LANES = 128
CONF_TOKEN_TILE = 512
MLP_TOKEN_TILE = 1024
SCONV_TOKEN_TILE = 512
CHANNEL_CHUNK = 512
FF_CHUNK = 1024
MOD_CHUNK = 1024
NORM_ROWS = 64
CONV_ROWS = 32
CONF_HALO = 32
SHORT_HALO = 8
VMEM_LIMIT_BYTES = 58 * 1024 * 1024


def _sigmoid(v):
    return 1.0 / (1.0 + jnp.exp(-v))


def _pre_norm_mod(h_ref, x_ref, gs_ref, pre_g_ref, shift_ref, scale_ref):
    gs_ref[...] = pre_g_ref[...] * (1.0 + scale_ref[...])
    for r0 in range(0, x_ref.shape[0], NORM_ROWS):
        x = x_ref[r0:r0 + NORM_ROWS, :]
        ms = jnp.mean(x * x, axis=-1, keepdims=True)
        h = (x * lax.rsqrt(ms + RMS_EPS)) * gs_ref[...] + shift_ref[...]
        h_ref[r0:r0 + NORM_ROWS, :] = h.astype(BF16)


def _post_norm_residual(o_ref, x_ref, y_ref, gs_ref, post_g_ref, gate_ref):
    gs_ref[...] = post_g_ref[...] * gate_ref[...]
    for r0 in range(0, x_ref.shape[0], NORM_ROWS):
        y = y_ref[r0:r0 + NORM_ROWS, :]
        ms = jnp.mean(y * y, axis=-1, keepdims=True)
        o_ref[r0:r0 + NORM_ROWS, :] = (x_ref[r0:r0 + NORM_ROWS, :]
                                       + gs_ref[...] * (y * lax.rsqrt(ms + RMS_EPS)))


def _mod_kernel(c_ref, w_ref, b_ref, o_ref):
    c = c_ref[...]
    c_act = (c * _sigmoid(c)).astype(BF16)
    w = w_ref[...].astype(BF16)
    o_ref[...] = jnp.dot(c_act, w, preferred_element_type=F32) + b_ref[...]


def _modulation(c, mod_w, mod_b):
    depth, d, n = mod_w.shape
    b = c.shape[0]
    rows = 8
    c_pad = jnp.zeros((rows, d), F32).at[:b].set(c)
    out = pl.pallas_call(
        _mod_kernel,
        grid=(depth, n // MOD_CHUNK),
        in_specs=[
            pl.BlockSpec((rows, d), lambda i, k: (0, 0)),
            pl.BlockSpec((None, d, MOD_CHUNK), lambda i, k: (i, 0, k)),
            pl.BlockSpec((None, 1, MOD_CHUNK), lambda i, k: (i, 0, k)),
        ],
        out_specs=pl.BlockSpec((None, rows, MOD_CHUNK), lambda i, k: (i, 0, k)),
        out_shape=jax.ShapeDtypeStruct((depth, rows, n), F32),
        compiler_params=pltpu.CompilerParams(
            dimension_semantics=("arbitrary", "arbitrary"),
            vmem_limit_bytes=VMEM_LIMIT_BYTES),
        name="adaln_modulation",
    )(c_pad, mod_w, mod_b.reshape(depth, 1, n))
    return out[:, :b].reshape(depth, b, N_MOD, d)


def _pre_rows(h_ref, slot, r0, x_ref, pre_g_ref, shift_ref, scale_ref):
    gs = pre_g_ref[...] * (1.0 + scale_ref[...])
    shift = shift_ref[...]
    for rr in range(0, x_ref.shape[0], NORM_ROWS):
        x = x_ref[rr:rr + NORM_ROWS, :]
        ms = jnp.mean(x * x, axis=-1, keepdims=True)
        h = (x * lax.rsqrt(ms + RMS_EPS)) * gs + shift
        h_ref[slot, pl.ds(r0 + rr, NORM_ROWS), :] = h.astype(BF16)


def _post_rows(o_ref, acc_ref, slot, r0, x_ref, post_g_ref, gate_ref):
    gg = post_g_ref[...] * gate_ref[...]
    for rr in range(0, x_ref.shape[0], NORM_ROWS):
        y = acc_ref[slot, pl.ds(r0 + rr, NORM_ROWS), :]
        ms = jnp.mean(y * y, axis=-1, keepdims=True)
        o_ref[rr:rr + NORM_ROWS, :] = (x_ref[rr:rr + NORM_ROWS, :]
                                       + gg * (y * lax.rsqrt(ms + RMS_EPS)))
        acc_ref[slot, pl.ds(r0 + rr, NORM_ROWS), :] = jnp.zeros((NORM_ROWS, y.shape[1]), F32)


def _skewed_steps(nt, init, pre, main, post):
    s = pl.program_id(0)
    j = pl.program_id(1)

    @pl.when(s == 0)
    def _():
        @pl.when(j == 0)
        def _():
            init()
        pre(0)

    for parity in (0, 1):
        @pl.when((s >= 1) & (s <= nt) & (lax.rem(s, 2) == parity))
        def _(parity=parity):
            post(parity)
            pre(parity)
            main(1 - parity)

    @pl.when(s == nt + 1)
    def _():
        post((nt + 1) % 2)


def _skewed_specs(nt, nj, rows, d, tiles_per_seq, layer):
    pre_tile = lambda s: jnp.minimum(s, nt - 1)
    post_tile = lambda s: jnp.clip(s - 2, 0, nt - 1)
    in_specs = [
        pl.BlockSpec((rows, d), lambda s, j: (pre_tile(s) * nj + j, 0)),
        pl.BlockSpec((rows, d), lambda s, j: (post_tile(s) * nj + j, 0)),
        pl.BlockSpec((None, None, N_MOD, d), lambda s, j: (layer, pre_tile(s) // tiles_per_seq, 0, 0)),
        pl.BlockSpec((None, None, N_MOD, d), lambda s, j: (layer, post_tile(s) // tiles_per_seq, 0, 0)),
        pl.BlockSpec((None, 1, d), lambda s, j: (layer, 0, 0)),
        pl.BlockSpec((None, 1, d), lambda s, j: (layer, 0, 0)),
    ]
    out_spec = pl.BlockSpec((rows, d), lambda s, j: (jnp.where(s < 2, 0, (s - 2) * nj + j), 0))
    chunk = lambda s, j: jnp.where(s == 0, 0, jnp.where(s == nt + 1, nj - 1, j))
    return in_specs, out_spec, chunk


def _mlp_kernel(nt, xn_ref, xp_ref, modn_ref, modp_ref, pre_g_ref, post_g_ref,
                w1_ref, w2_ref, o_ref, h_ref, acc_ref):
    rows = xn_ref.shape[0]
    r0 = pl.multiple_of(pl.program_id(1) * rows, rows)

    def init():
        acc_ref[...] = jnp.zeros_like(acc_ref)

    def pre(slot):
        _pre_rows(h_ref, slot, r0, xn_ref, pre_g_ref, modn_ref.at[3:4, :], modn_ref.at[4:5, :])

    def post(slot):
        _post_rows(o_ref, acc_ref, slot, r0, xp_ref, post_g_ref, modp_ref.at[5:6, :])

    def main(slot):
        u = jnp.dot(h_ref[slot], w1_ref[...], preferred_element_type=F32)
        u = jnp.maximum(u, 0.0)
        u = (u * u).astype(BF16)
        acc_ref[slot] += jnp.dot(u, w2_ref[...], preferred_element_type=F32)

    _skewed_steps(nt, init, pre, main, post)


def _mlp_sublayer(x, mod, pre_g, post_g, w1, w2, layer, seq_len):
    n, d = x.shape
    dff = w1.shape[-1]
    tm, tf = MLP_TOKEN_TILE, FF_CHUNK
    tiles_per_seq = seq_len // tm
    nt, nj = n // tm, dff // tf
    rows = tm // nj
    in_specs, out_spec, chunk = _skewed_specs(nt, nj, rows, d, tiles_per_seq, layer)
    in_specs += [
        pl.BlockSpec((None, d, tf), lambda s, j: (layer, 0, chunk(s, j))),
        pl.BlockSpec((None, tf, d), lambda s, j: (layer, chunk(s, j), 0)),
    ]
    return pl.pallas_call(
        functools.partial(_mlp_kernel, nt),
        grid=(nt + 2, nj),
        in_specs=in_specs,
        out_specs=out_spec,
        out_shape=jax.ShapeDtypeStruct(x.shape, F32),
        scratch_shapes=[pltpu.VMEM((2, tm, d), BF16), pltpu.VMEM((2, tm, d), F32)],
        compiler_params=pltpu.CompilerParams(
            dimension_semantics=("arbitrary", "arbitrary"),
            vmem_limit_bytes=VMEM_LIMIT_BYTES),
        name="mlp_sublayer",
    )(x, x, mod, mod, pre_g, post_g, w1, w2)


def _sconv_kernel(nt, tiles_per_seq, xn_ref, xp_ref, modn_ref, modp_ref, pre_g_ref, post_g_ref,
                  wb_ref, wc_ref, wx_ref, cw_ref, wo_ref, o_ref,
                  h_ref, acc_ref, v_ref, tail_ref, p_ref):
    s = pl.program_id(0)
    j = pl.program_id(1)
    rows = xn_ref.shape[0]
    r0 = pl.multiple_of(j * rows, rows)
    tm = h_ref.shape[1]
    halo = SHORT_HALO
    nl = v_ref.shape[0]

    def init():
        acc_ref[...] = jnp.zeros_like(acc_ref)
        tail_ref[...] = jnp.zeros_like(tail_ref)

    def pre(slot):
        _pre_rows(h_ref, slot, r0, xn_ref, pre_g_ref, modn_ref.at[0:1, :], modn_ref.at[1:2, :])

    def post(slot):
        _post_rows(o_ref, acc_ref, slot, r0, xp_ref, post_g_ref, modp_ref.at[2:3, :])

    def main(slot):
        h = h_ref[slot]
        v = (jnp.dot(h, wc_ref[...], preferred_element_type=F32)
             * jnp.dot(h, wx_ref[...], preferred_element_type=F32))
        zb = jnp.dot(h, wb_ref[...], preferred_element_type=F32)
        sequence_start = lax.rem(s - 1, tiles_per_seq) == 0
        for c in range(nl):
            cols = slice(c * LANES, (c + 1) * LANES)
            vc = v[:, cols]
            v_ref[c, 0:halo, :] = jnp.where(sequence_start, 0.0, tail_ref[j, c])
            v_ref[c, halo:, :] = vc
            tail_ref[j, c] = vc[tm - halo:, :]
            u = (cw_ref[0:1, cols] * v_ref[c, halo - 2:halo - 2 + tm, :]
                 + cw_ref[1:2, cols] * v_ref[c, halo - 1:halo - 1 + tm, :]
                 + cw_ref[2:3, cols] * vc)
            p_ref[:, cols] = (zb[:, cols] * u).astype(BF16)
        acc_ref[slot] += jnp.dot(p_ref[...], wo_ref[...], preferred_element_type=F32)

    _skewed_steps(nt, init, pre, main, post)


def _sconv_sublayer(x, mod, pre_g, post_g, w_in, w_conv, w_out, layer, mixer, seq_len):
    n, d = x.shape
    tm, tc = SCONV_TOKEN_TILE, CHANNEL_CHUNK
    tiles_per_seq = seq_len // tm
    nt, nj = n // tm, d // tc
    rows = tm // nj
    nl = tc // LANES
    taps = w_conv.shape[1]
    in_specs, out_spec, chunk = _skewed_specs(nt, nj, rows, d, tiles_per_seq, layer)
    in_specs += [
        pl.BlockSpec((None, d, tc), lambda s, j: (mixer, 0, chunk(s, j))),
        pl.BlockSpec((None, d, tc), lambda s, j: (mixer, 0, nj + chunk(s, j))),
        pl.BlockSpec((None, d, tc), lambda s, j: (mixer, 0, 2 * nj + chunk(s, j))),
        pl.BlockSpec((None, taps, tc), lambda s, j: (mixer, 0, chunk(s, j))),
        pl.BlockSpec((None, tc, d), lambda s, j: (mixer, chunk(s, j), 0)),
    ]
    return pl.pallas_call(
        functools.partial(_sconv_kernel, nt, tiles_per_seq),
        grid=(nt + 2, nj),
        in_specs=in_specs,
        out_specs=out_spec,
        out_shape=jax.ShapeDtypeStruct(x.shape, F32),
        scratch_shapes=[
            pltpu.VMEM((2, tm, d), BF16),
            pltpu.VMEM((2, tm, d), F32),
            pltpu.VMEM((nl, SHORT_HALO + tm, LANES), F32),
            pltpu.VMEM((nj, nl, SHORT_HALO, LANES), F32),
            pltpu.VMEM((tm, tc), BF16),
        ],
        compiler_params=pltpu.CompilerParams(
            dimension_semantics=("arbitrary", "arbitrary"),
            vmem_limit_bytes=VMEM_LIMIT_BYTES),
        name="short_conv_sublayer",
    )(x, x, mod, mod, pre_g, post_g, w_in, w_in, w_in, w_conv, w_out)


def _conf_kernel(x_ref, mod_ref, pre_g_ref, post_g_ref, w1_ref, b1_ref, dw_ref, dwb_ref,
                 lng_ref, lnb_ref, w2_ref, b2_ref, o_ref, h_ref, gs_ref, gx_ref):
    t = pl.program_id(1)
    tm, d = x_ref.shape
    nl = gx_ref.shape[0]
    halo = CONF_HALO
    taps = dw_ref.shape[0]
    tc = CHANNEL_CHUNK
    base = halo - (taps - 1)

    @pl.when(t == 0)
    def _():
        gx_ref[:, tm:tm + halo, :] = jnp.zeros((nl, halo, LANES), F32)

    _pre_norm_mod(h_ref, x_ref, gs_ref, pre_g_ref, mod_ref.at[0:1, :], mod_ref.at[1:2, :])

    for jj in range(d // tc):
        h = h_ref[...]
        a = (jnp.dot(h, w1_ref[:, jj * tc:(jj + 1) * tc], preferred_element_type=F32)
             + b1_ref[:, jj * tc:(jj + 1) * tc])
        g = (jnp.dot(h, w1_ref[:, d + jj * tc:d + (jj + 1) * tc], preferred_element_type=F32)
             + b1_ref[:, d + jj * tc:d + (jj + 1) * tc])
        glu = a * _sigmoid(g)
        for c in range(tc // LANES):
            lane = jj * (tc // LANES) + c
            cols = slice(lane * LANES, (lane + 1) * LANES)
            gx_ref[lane, 0:halo, :] = gx_ref[lane, tm:tm + halo, :]
            gx_ref[lane, halo:, :] = glu[:, c * LANES:(c + 1) * LANES]
            for r0 in range(0, tm, CONV_ROWS):
                acc = jnp.zeros((CONV_ROWS, LANES), F32)
                for k in range(taps):
                    acc = acc + dw_ref[k:k + 1, cols] * gx_ref[lane, r0 + base + k:r0 + base + k + CONV_ROWS, :]
                o_ref[r0:r0 + CONV_ROWS, cols] = acc + dwb_ref[:, cols]

    for r0 in range(0, tm, NORM_ROWS):
        cv = o_ref[r0:r0 + NORM_ROWS, :]
        mu = jnp.mean(cv, axis=-1, keepdims=True)
        dev = cv - mu
        var = jnp.mean(dev * dev, axis=-1, keepdims=True)
        z = dev * lax.rsqrt(var + LN_EPS) * lng_ref[...] + lnb_ref[...]
        h_ref[r0:r0 + NORM_ROWS, :] = (z * _sigmoid(z)).astype(BF16)

    o_ref[...] = jnp.dot(h_ref[...], w2_ref[...], preferred_element_type=F32) + b2_ref[...]
    _post_norm_residual(o_ref, x_ref, o_ref, gs_ref, post_g_ref, mod_ref.at[2:3, :])


def _conf_sublayer(x, mod, pre_g, post_g, w1, b1, dw, dwb, ln_g, ln_b, w2, b2, layer, mixer):
    b, s, d = x.shape
    tm = CONF_TOKEN_TILE
    taps = dw.shape[1]
    row = lambda bi, ti: (layer, 0, 0)
    mrow = lambda bi, ti: (mixer, 0, 0)
    resident = dict(pipeline_mode=pl.Buffered(1))
    return pl.pallas_call(
        _conf_kernel,
        grid=(b, s // tm),
        in_specs=[
            pl.BlockSpec((None, tm, d), lambda bi, ti: (bi, ti, 0)),
            pl.BlockSpec((None, None, N_MOD, d), lambda bi, ti: (layer, bi, 0, 0)),
            pl.BlockSpec((None, 1, d), row),
            pl.BlockSpec((None, 1, d), row),
            pl.BlockSpec((None, d, 2 * d), mrow, **resident),
            pl.BlockSpec((None, 1, 2 * d), mrow),
            pl.BlockSpec((None, taps, d), mrow),
            pl.BlockSpec((None, 1, d), mrow),
            pl.BlockSpec((None, 1, d), mrow),
            pl.BlockSpec((None, 1, d), mrow),
            pl.BlockSpec((None, d, d), mrow, **resident),
            pl.BlockSpec((None, 1, d), mrow),
        ],
        out_specs=pl.BlockSpec((None, tm, d), lambda bi, ti: (bi, ti, 0)),
        out_shape=jax.ShapeDtypeStruct(x.shape, F32),
        scratch_shapes=[
            pltpu.VMEM((tm, d), BF16),
            pltpu.VMEM((1, d), F32),
            pltpu.VMEM((d // LANES, CONF_HALO + tm, LANES), F32),
        ],
        compiler_params=pltpu.CompilerParams(
            dimension_semantics=("arbitrary", "arbitrary"),
            vmem_limit_bytes=VMEM_LIMIT_BYTES),
        name="conformer_sublayer",
    )(x, mod, pre_g, post_g, w1, b1, dw, dwb, ln_g, ln_b, w2, b2)


def kernel(x, c, mod_w, mod_b, pre_mix_g, post_mix_g, pre_ffn_g, post_ffn_g, a_w1, a_b1, a_dw, a_dwb, a_ln_g, a_ln_b, a_w2, a_b2, b_w_in, b_conv, b_w_out, f_w1, f_w2):
    depth = mod_w.shape[0]
    b, s, d = x.shape
    mod = _modulation(c, mod_w, mod_b)
    rows = lambda p: p[:, None, :]
    pre_mix_g, post_mix_g = rows(pre_mix_g), rows(post_mix_g)
    pre_ffn_g, post_ffn_g = rows(pre_ffn_g), rows(post_ffn_g)
    a_b1, a_dwb, a_ln_g, a_ln_b, a_b2 = map(rows, (a_b1, a_dwb, a_ln_g, a_ln_b, a_b2))
    a_w1, a_w2 = a_w1.astype(BF16), a_w2.astype(BF16)
    b_w_in, b_w_out = b_w_in.astype(BF16), b_w_out.astype(BF16)
    f_w1, f_w2 = f_w1.astype(BF16), f_w2.astype(BF16)
    for i in range(depth):
        j = i // 2
        if i % 2 == 0:
            x = _conf_sublayer(x.reshape(b, s, d), mod, pre_mix_g, post_mix_g, a_w1, a_b1, a_dw,
                               a_dwb, a_ln_g, a_ln_b, a_w2, a_b2, i, j)
        else:
            x = _sconv_sublayer(x.reshape(b * s, d), mod, pre_mix_g, post_mix_g, b_w_in, b_conv,
                                b_w_out, i, j, s)
        x = _mlp_sublayer(x.reshape(b * s, d), mod, pre_ffn_g, post_ffn_g, f_w1, f_w2, i, s)
    return x.reshape(b, s, d)
```
